```python
import math
import jax, jax.numpy as jnp
from jax import lax
import numpy as np

D_MODEL = 1024
BATCH = 8
SEQ = 2048
DEPTH = 2
DEC_BATCH = 128
DEC_SEQ = 8
PAST_LEN = 16384
PAGE_SIZE = 128

N_EVEN = (DEPTH + 1) // 2
N_ODD = DEPTH // 2
D_POOL = D_MODEL // 2
POOL_WINDOWS = (2, 4, 8, 16)
N_POOL_GROUPS = len(POOL_WINDOWS)
POOL_GROUP = D_POOL // N_POOL_GROUPS
POOL_BUF = max(POOL_WINDOWS) - 1
D_RNN = D_MODEL // 2
RNN_HEADS = 8
RNN_HEAD_DIM = D_RNN // RNN_HEADS
RNN_CONV = 4
RG_C = 8.0
D_IN_EVEN = D_POOL + 2 * D_RNN
D_CONV = D_MODEL
CONV_WIDTH = 31
PEER_HEADS = 8
PEER_NKEYS = 128
PEER_N = PEER_NKEYS * PEER_NKEYS
PEER_DQ = 256
PEER_DH = PEER_DQ // 2
PEER_TOPK = 16
PEER_BLOCK = 128
LN_EPS = 1e-5
ALPHA = (2.0 * DEPTH) ** 0.25
BETA = (8.0 * DEPTH) ** -0.25

kernel_name = 'hybrid_pool_rglru_conformer_peer_step'


def _layer_norm(x, g, b):
    xf = x.astype(jnp.float32)
    mu = jnp.mean(xf, axis=-1, keepdims=True)
    var = jnp.mean(jnp.square(xf - mu), axis=-1, keepdims=True)
    y = (xf - mu) * lax.rsqrt(var + LN_EPS) * g.astype(jnp.float32) + b.astype(jnp.float32)
    return y.astype(x.dtype)


def _causal_dwconv(x_ext, w, b):
    c = x_ext.shape[-1]
    y = lax.conv_general_dilated(x_ext, w[:, None, :].astype(x_ext.dtype), (1,), 'VALID',
                                 dimension_numbers=('NWC', 'WIO', 'NWC'), feature_group_count=c)
    return y + b.astype(x_ext.dtype)


def _multiscale_pool(xa_ext, start):
    bsz, length, _ = xa_ext.shape
    s = length - POOL_BUF
    xg = xa_ext.astype(jnp.float32).reshape(bsz, length, N_POOL_GROUPS, POOL_GROUP)
    csum = jnp.concatenate([jnp.zeros_like(xg[:, :1]), jnp.cumsum(xg, axis=1)], axis=1)
    pos = start + jnp.arange(s)
    means = []
    for g, w in enumerate(POOL_WINDOWS):
        win = csum[:, POOL_BUF + 1:POOL_BUF + 1 + s, g] - csum[:, POOL_BUF + 1 - w:POOL_BUF + 1 - w + s, g]
        cnt = jnp.minimum(pos + 1, w).astype(jnp.float32)
        means.append(win / cnt[None, :, None])
    mean = jnp.stack(means, axis=2)
    return (mean - xg[:, POOL_BUF:]).astype(xa_ext.dtype)


def _rglru(xc, h0, wa, ba, wx, bx, lam):
    bsz, s, c = xc.shape
    xf = xc.astype(jnp.float32)
    xh = xf.reshape(bsz, s, RNN_HEADS, RNN_HEAD_DIM)
    r = jax.nn.sigmoid(jnp.einsum('bshi,hij->bshj', xh, wa.astype(jnp.float32)).reshape(bsz, s, c) + ba.astype(jnp.float32))
    i = jax.nn.sigmoid(jnp.einsum('bshi,hij->bshj', xh, wx.astype(jnp.float32)).reshape(bsz, s, c) + bx.astype(jnp.float32))
    log_a = -RG_C * r * jax.nn.softplus(-lam.astype(jnp.float32))
    a = jnp.exp(log_a)
    u = jnp.sqrt(-jnp.expm1(2.0 * log_a)) * (i * xf)
    u = u.at[:, 0].add(a[:, 0] * h0.astype(jnp.float32))

    def combine(p, q):
        return p[0] * q[0], q[0] * p[1] + q[1]

    _, h = lax.associative_scan(combine, (a, u), axis=1)
    h = h.astype(xc.dtype)
    return h, h[:, -1]


def _even_mixer(x, pool_buf, conv_buf, h0, start, w_in, pool_w, pool_scale, conv_w, conv_b, wa, ba, wx, bx, lam, w_out):
    bsz, s, _ = x.shape
    z = x @ w_in
    xa = z[..., :D_POOL]
    xb = z[..., D_POOL:D_POOL + D_RNN]
    xg = z[..., D_POOL + D_RNN:]
    xa_ext = jnp.concatenate([pool_buf.astype(xa.dtype), xa], axis=1)
    pooled = _multiscale_pool(xa_ext, start)
    ya = jnp.einsum('bsgc,gcd->bsgd', pooled, pool_w).reshape(bsz, s, D_POOL) * pool_scale
    xb_ext = jnp.concatenate([conv_buf.astype(xb.dtype), xb], axis=1)
    xc = _causal_dwconv(xb_ext, conv_w, conv_b)
    h, h_last = _rglru(xc, h0, wa, ba, wx, bx, lam)
    yb = h * jax.nn.gelu(xg)
    out = jnp.concatenate([ya, yb], axis=-1) @ w_out
    return out, xa_ext[:, -POOL_BUF:], xb_ext[:, -(RNN_CONV - 1):], h_last


def _odd_mixer(x, conv_buf, w_glu, dw_w, dw_b, ln_g, ln_b, w_pw):
    z = x @ w_glu
    g = z[..., :D_CONV] * jax.nn.sigmoid(z[..., D_CONV:])
    g_ext = jnp.concatenate([conv_buf.astype(g.dtype), g], axis=1)
    c = _causal_dwconv(g_ext, dw_w, dw_b)
    c = jax.nn.silu(_layer_norm(c, ln_g, ln_b))
    return c @ w_pw, g_ext[:, -(CONV_WIDTH - 1):]


def _peer_block(xb, wq, k1, k2, u, v):
    n = xb.shape[0]
    q = (xb @ wq).reshape(n, PEER_HEADS, PEER_DQ).astype(jnp.float32)
    s1 = jnp.einsum('nhd,kd->nhk', q[..., :PEER_DH], k1.astype(jnp.float32))
    s2 = jnp.einsum('nhd,kd->nhk', q[..., PEER_DH:], k2.astype(jnp.float32))
    v1, i1 = lax.top_k(s1, PEER_TOPK)
    v2, i2 = lax.top_k(s2, PEER_TOPK)
    cand = (v1[..., :, None] + v2[..., None, :]).reshape(n, PEER_HEADS, PEER_TOPK * PEER_TOPK)
    sv, ci = lax.top_k(cand, PEER_TOPK)
    e = (jnp.take_along_axis(i1, ci // PEER_TOPK, axis=-1) * PEER_NKEYS
         + jnp.take_along_axis(i2, ci % PEER_TOPK, axis=-1))
    gate = jax.nn.softmax(sv, axis=-1)
    act = jax.nn.gelu(jnp.einsum('nhkd,nd->nhk', u[e], xb).astype(jnp.float32))
    wgt = (gate * act).astype(xb.dtype)
    return jnp.einsum('nhk,nhkd->nd', wgt, v[e])


def _peer(x, wq, k1, k2, u, v):
    bsz, s, d = x.shape
    t = bsz * s
    nb = -(-t // PEER_BLOCK)
    xt = jnp.pad(x.reshape(t, d), ((0, nb * PEER_BLOCK - t), (0, 0))).reshape(nb, PEER_BLOCK, d)
    y = lax.map(lambda xb: _peer_block(xb, wq, k1, k2, u, v), xt)
    return y.reshape(nb * PEER_BLOCK, d)[:t].reshape(bsz, s, d)


def _trunk(x, st_pool, st_rconv, st_h, st_conv, start, p):
    pools, rconvs, hs, convs = [], [], [], []
    for layer in range(DEPTH):
        j = layer // 2
        if layer % 2 == 0:
            m, pb, rb, hl = _even_mixer(x, st_pool[j], st_rconv[j], st_h[j], start,
                                        p['w_in_even'][j], p['pool_w'][j], p['pool_scale'][j],
                                        p['rnn_conv_w'][j], p['rnn_conv_b'][j], p['rg_wa'][j], p['rg_ba'][j],
                                        p['rg_wx'][j], p['rg_bx'][j], p['rg_lambda'][j], p['w_out_even'][j])
            pools.append(pb)
            rconvs.append(rb)
            hs.append(hl)
        else:
            m, cb = _odd_mixer(x, st_conv[j], p['w_glu'][j], p['dw_w'][j], p['dw_b'][j],
                               p['conv_ln_g'][j], p['conv_ln_b'][j], p['w_pw'][j])
            convs.append(cb)
        x = _layer_norm(ALPHA * x + m, p['ln_mix_g'][layer], p['ln_mix_b'][layer])
        f = _peer(x, p['peer_wq'][layer], p['peer_k1'][layer], p['peer_k2'][layer],
                  p['peer_u'][layer], p['peer_v'][layer])
        x = _layer_norm(ALPHA * x + f, p['ln_ffn_g'][layer], p['ln_ffn_b'][layer])
    return x, jnp.stack(pools), jnp.stack(rconvs), jnp.stack(hs), jnp.stack(convs)


def setup_inputs(seed: int = 0) -> dict:
    key = jax.random.key(seed)
    ks = jax.random.split(key, 40)
    f32 = jnp.float32

    def nrm(k, shape, scale):
        return jax.random.normal(k, shape, f32) * scale

    def gain(k, shape):
        return 1.0 + 0.05 * jax.random.normal(k, shape, f32)

    a_init = jax.random.uniform(ks[15], (N_EVEN, D_RNN), f32, 0.9, 0.999)
    s_init = a_init ** (1.0 / RG_C)
    return {
        'x_prompt': nrm(ks[0], (BATCH, SEQ, D_MODEL), 1.0),
        'x_sample': nrm(ks[1], (DEC_BATCH, DEC_SEQ, D_MODEL), 1.0),
        'state_pool': nrm(ks[2], (N_EVEN, DEC_BATCH, POOL_BUF, D_POOL), 1.0),
        'state_rnn_conv': nrm(ks[3], (N_EVEN, DEC_BATCH, RNN_CONV - 1, D_RNN), 1.0),
        'state_rnn_h': nrm(ks[4], (N_EVEN, DEC_BATCH, D_RNN), 0.5),
        'state_conv': nrm(ks[5], (N_ODD, DEC_BATCH, CONV_WIDTH - 1, D_CONV), 1.0),
        'w_in_even': nrm(ks[6], (N_EVEN, D_MODEL, D_IN_EVEN), D_MODEL ** -0.5),
        'pool_w': nrm(ks[7], (N_EVEN, N_POOL_GROUPS, POOL_GROUP, POOL_GROUP), POOL_GROUP ** -0.5),
        'pool_scale': gain(ks[8], (N_EVEN, D_POOL)),
        'rnn_conv_w': nrm(ks[9], (N_EVEN, RNN_CONV, D_RNN), RNN_CONV ** -0.5),
        'rnn_conv_b': nrm(ks[10], (N_EVEN, D_RNN), 0.02),
        'rg_wa': nrm(ks[11], (N_EVEN, RNN_HEADS, RNN_HEAD_DIM, RNN_HEAD_DIM), RNN_HEAD_DIM ** -0.5),
        'rg_ba': nrm(ks[12], (N_EVEN, D_RNN), 0.02),
        'rg_wx': nrm(ks[13], (N_EVEN, RNN_HEADS, RNN_HEAD_DIM, RNN_HEAD_DIM), RNN_HEAD_DIM ** -0.5),
        'rg_bx': nrm(ks[14], (N_EVEN, D_RNN), 0.02),
        'rg_lambda': jnp.log(s_init) - jnp.log1p(-s_init),
        'w_out_even': nrm(ks[16], (N_EVEN, D_POOL + D_RNN, D_MODEL), BETA * (D_POOL + D_RNN) ** -0.5),
        'w_glu': nrm(ks[17], (N_ODD, D_MODEL, 2 * D_CONV), D_MODEL ** -0.5),
        'dw_w': nrm(ks[18], (N_ODD, CONV_WIDTH, D_CONV), CONV_WIDTH ** -0.5),
        'dw_b': nrm(ks[19], (N_ODD, D_CONV), 0.02),
        'conv_ln_g': gain(ks[20], (N_ODD, D_CONV)),
        'conv_ln_b': nrm(ks[21], (N_ODD, D_CONV), 0.02),
        'w_pw': nrm(ks[22], (N_ODD, D_CONV, D_MODEL), BETA * D_CONV ** -0.5),
        'peer_wq': nrm(ks[23], (DEPTH, D_MODEL, PEER_HEADS * PEER_DQ), D_MODEL ** -0.5),
        'peer_k1': nrm(ks[24], (DEPTH, PEER_NKEYS, PEER_DH), PEER_DH ** -0.5),
        'peer_k2': nrm(ks[25], (DEPTH, PEER_NKEYS, PEER_DH), PEER_DH ** -0.5),
        'peer_u': nrm(ks[26], (DEPTH, PEER_N, D_MODEL), D_MODEL ** -0.5),
        'peer_v': nrm(ks[27], (DEPTH, PEER_N, D_MODEL), BETA),
        'ln_mix_g': gain(ks[28], (DEPTH, D_MODEL)),
        'ln_mix_b': nrm(ks[29], (DEPTH, D_MODEL), 0.02),
        'ln_ffn_g': gain(ks[30], (DEPTH, D_MODEL)),
        'ln_ffn_b': nrm(ks[31], (DEPTH, D_MODEL), 0.02),
    }


def reference(x_prompt, x_sample, state_pool, state_rnn_conv, state_rnn_h, state_conv,
              w_in_even, pool_w, pool_scale, rnn_conv_w, rnn_conv_b, rg_wa, rg_ba, rg_wx, rg_bx,
              rg_lambda, w_out_even, w_glu, dw_w, dw_b, conv_ln_g, conv_ln_b, w_pw,
              peer_wq, peer_k1, peer_k2, peer_u, peer_v, ln_mix_g, ln_mix_b, ln_ffn_g, ln_ffn_b):
    p = dict(w_in_even=w_in_even, pool_w=pool_w, pool_scale=pool_scale, rnn_conv_w=rnn_conv_w,
             rnn_conv_b=rnn_conv_b, rg_wa=rg_wa, rg_ba=rg_ba, rg_wx=rg_wx, rg_bx=rg_bx,
             rg_lambda=rg_lambda, w_out_even=w_out_even, w_glu=w_glu, dw_w=dw_w, dw_b=dw_b,
             conv_ln_g=conv_ln_g, conv_ln_b=conv_ln_b, w_pw=w_pw, peer_wq=peer_wq,
             peer_k1=peer_k1, peer_k2=peer_k2, peer_u=peer_u, peer_v=peer_v,
             ln_mix_g=ln_mix_g, ln_mix_b=ln_mix_b, ln_ffn_g=ln_ffn_g, ln_ffn_b=ln_ffn_b)
    bp = x_prompt.shape[0]
    dt = x_prompt.dtype
    y_prompt, pool_p, rconv_p, h_p, conv_p = _trunk(
        x_prompt,
        jnp.zeros((N_EVEN, bp, POOL_BUF, D_POOL), dt),
        jnp.zeros((N_EVEN, bp, RNN_CONV - 1, D_RNN), dt),
        jnp.zeros((N_EVEN, bp, D_RNN), dt),
        jnp.zeros((N_ODD, bp, CONV_WIDTH - 1, D_CONV), dt),
        0, p)
    y_sample, pool_s, rconv_s, h_s, conv_s = _trunk(
        x_sample, state_pool, state_rnn_conv, state_rnn_h, state_conv, PAST_LEN, p)
    return (y_prompt, y_sample, pool_p, rconv_p, h_p, conv_p, pool_s, rconv_s, h_s, conv_s)
```

```python
import functools

import jax
import jax.numpy as jnp
from jax import lax
from jax.experimental import pallas as pl
from jax.experimental.pallas import tpu as pltpu

F32 = jnp.float32
BF16 = jnp.bfloat16

DEPTH = 2
PAST_LEN = 16384
POOL_WINDOWS = (2, 4, 8, 16)
RG_C = 8.0
PEER_TOPK = 16
LN_EPS = 1e-5
ALPHA = (2.0 * DEPTH) ** 0.25

LANES = 128
SUBLANES = 8
BF16_ROWS = 16
VMEM_LIMIT_BYTES = 56 * 1024 * 1024

PROMPT_TS = 512
SAMPLE_NB = 32
CONV_ROW_BLOCK = 128
PEER_TN = 512
PEER_TE = 1024


def _layer_norm(v, g, b):
    mu = jnp.mean(v, axis=-1, keepdims=True)
    d = v - mu
    var = jnp.mean(d * d, axis=-1, keepdims=True)
    return d * lax.rsqrt(var + LN_EPS) * g + b


def _gelu_tanh(x):
    return 0.5 * x * (1.0 + jnp.tanh(0.7978845608028654 * (x + 0.044715 * (x * x * x))))


def _sigmoid(x):
    return 1.0 / (1.0 + jnp.exp(-x))


def _mm(a, b):
    return jnp.dot(a, b, preferred_element_type=F32)


def _mm_nt(a, b):
    return lax.dot_general(a, b, (((1,), (1,)), ((), ())), preferred_element_type=F32)


def _even_kernel(start, nb, ts, pad,
                 x_ref, pool0_ref, rconv0_ref, h0_ref, w_in_ref, pool_w_ref, pool_scale_ref,
                 cw_ref, cb_ref, wa_ref, ba_ref, wx_ref, bx_ref, lam_ref, w_out_ref,
                 lng_ref, lnb_ref,
                 y_ref, pool_out_ref, rconv_out_ref, h_out_ref,
                 pa_ref, cv_ref, sa_ref, su_ref, cat_ref, hc_ref):
    t = pl.program_id(1)
    d_model = x_ref.shape[-1]
    d_pool = pool0_ref.shape[-1]
    d_rnn = rconv0_ref.shape[-1]
    p_pool = pool0_ref.shape[1]
    p_conv = rconv0_ref.shape[1]
    hp = pa_ref.shape[1] - ts
    hc = cv_ref.shape[1] - ts
    m = nb * ts
    group = d_pool // len(POOL_WINDOWS)

    @pl.when(t == 0)
    def _():
        pa_ref[:, hp - p_pool:hp, :] = pool0_ref[...]
        cv_ref[:, hc - p_conv:hc, :] = rconv0_ref[...]
        hc_ref[...] = h0_ref[...]

    x = x_ref[...].reshape(m, d_model)
    z = _mm(x.astype(BF16), w_in_ref[...])
    pa_ref[:, hp:hp + ts, :] = z[:, :d_pool].reshape(nb, ts, d_pool)
    cv_ref[:, hc:hc + ts, :] = z[:, d_pool:d_pool + d_rnn].reshape(nb, ts, d_rnn)
    xg = z[:, d_pool + d_rnn:]

    pos = start + t * ts + lax.broadcasted_iota(jnp.int32, (nb, ts, group), 1)
    for g, w in enumerate(POOL_WINDOWS):
        cs = slice(g * group, (g + 1) * group)
        tok = pa_ref[:, hp:hp + ts, cs]
        win = tok
        for d in range(1, w):
            win = win + pa_ref[:, hp - d:hp - d + ts, cs]
        cnt = jnp.minimum(pos + 1, w).astype(F32)
        pooled = (win / cnt - tok).reshape(m, group)
        ya = _mm(pooled.astype(BF16), pool_w_ref[g]) * pool_scale_ref[:, cs]
        cat_ref[:, cs] = ya.astype(BF16)

    kw = cw_ref.shape[0]
    xc = cv_ref[:, hc - p_conv:hc - p_conv + ts, :] * cw_ref[0:1, :]
    for k in range(1, kw):
        xc = xc + cv_ref[:, hc - p_conv + k:hc - p_conv + k + ts, :] * cw_ref[k:k + 1, :]
    xc = (xc + cb_ref[...]).reshape(m, d_rnn)
    xcb = xc.astype(BF16)
    r = _sigmoid(_mm(xcb, wa_ref[...]) + ba_ref[...])
    i = _sigmoid(_mm(xcb, wx_ref[...]) + bx_ref[...])
    nl = -lam_ref[...]
    softplus = jnp.maximum(nl, 0.0) + jnp.log(1.0 + jnp.exp(-jnp.abs(nl)))
    log_a = (-RG_C) * r * softplus
    a = jnp.exp(log_a)
    u = jnp.sqrt(jnp.tanh(-log_a) * (a * a + 1.0)) * (i * xc)

    sa_ref[:, :, :pad, :] = jnp.ones((2, nb, pad, d_rnn), F32)
    su_ref[:, :, :pad, :] = jnp.zeros((2, nb, pad, d_rnn), F32)
    sa_ref[0, :, pad:, :] = a.reshape(nb, ts, d_rnn)
    su_ref[0, :, pad:, :] = u.reshape(nb, ts, d_rnn)
    src = 0
    k = 1
    while k < ts:
        dst = 1 - src
        a_cur = sa_ref[src, :, pad:, :]
        su_ref[dst, :, pad:, :] = su_ref[src, :, pad:, :] + a_cur * su_ref[src, :, pad - k:pad - k + ts, :]
        sa_ref[dst, :, pad:, :] = a_cur * sa_ref[src, :, pad - k:pad - k + ts, :]
        src = dst
        k *= 2
    h = su_ref[src, :, pad:, :] + sa_ref[src, :, pad:, :] * hc_ref[...]
    hc_ref[...] = h[:, ts - 1:ts, :]
    h_out_ref[...] = h[:, ts - 1:ts, :]

    cat_ref[:, d_pool:] = (h.reshape(m, d_rnn) * _gelu_tanh(xg)).astype(BF16)
    out = _mm(cat_ref[...], w_out_ref[...])
    y = _layer_norm(ALPHA * x + out, lng_ref[...], lnb_ref[...])
    y_ref[...] = y.reshape(nb, ts, d_model)

    pool_out_ref[...] = pa_ref[:, ts + hp - p_pool:ts + hp, :]
    rconv_out_ref[...] = cv_ref[:, ts + hc - p_conv:ts + hc, :]
    pa_ref[:, 0:hp, :] = pa_ref[:, ts:ts + hp, :]
    cv_ref[:, 0:hc, :] = cv_ref[:, ts:ts + hc, :]


def _seq_tiling(bsz, s):
    if s > PROMPT_TS:
        assert s % PROMPT_TS == 0
        return 1, PROMPT_TS
    nb = min(bsz, SAMPLE_NB)
    assert bsz % nb == 0
    return nb, s


def _full_spec(arr):
    nd = arr.ndim
    return pl.BlockSpec(arr.shape, lambda b, t, _n=nd: (0,) * _n)


def _even_mixer(x, pool0, rconv0, h0, start, w):
    bsz, s, d_model = x.shape
    nb, ts = _seq_tiling(bsz, s)
    d_pool = pool0.shape[-1]
    d_rnn = rconv0.shape[-1]
    p_pool, p_conv = pool0.shape[1], rconv0.shape[1]
    hp, hc = 2 * SUBLANES, SUBLANES
    pad = max(SUBLANES, ts // 2)
    h0 = h0.reshape(bsz, 1, d_rnn)
    weights = (w['w_in'], w['pool_w'], w['pool_scale'], w['conv_w'], w['conv_b'], w['wa'], w['ba'],
               w['wx'], w['bx'], w['lam'], w['w_out'], w['ln_g'], w['ln_b'])
    seq3 = lambda b, t: (b, 0, 0)
    y, pool, rconv, h = pl.pallas_call(
        functools.partial(_even_kernel, start, nb, ts, pad),
        grid=(bsz // nb, s // ts),
        in_specs=[pl.BlockSpec((nb, ts, d_model), lambda b, t: (b, t, 0)),
                  pl.BlockSpec((nb, p_pool, d_pool), seq3),
                  pl.BlockSpec((nb, p_conv, d_rnn), seq3),
                  pl.BlockSpec((nb, 1, d_rnn), seq3)] + [_full_spec(a) for a in weights],
        out_specs=[pl.BlockSpec((nb, ts, d_model), lambda b, t: (b, t, 0)),
                   pl.BlockSpec((nb, p_pool, d_pool), seq3),
                   pl.BlockSpec((nb, p_conv, d_rnn), seq3),
                   pl.BlockSpec((nb, 1, d_rnn), seq3)],
        out_shape=[jax.ShapeDtypeStruct((bsz, s, d_model), F32),
                   jax.ShapeDtypeStruct((bsz, p_pool, d_pool), F32),
                   jax.ShapeDtypeStruct((bsz, p_conv, d_rnn), F32),
                   jax.ShapeDtypeStruct((bsz, 1, d_rnn), F32)],
        scratch_shapes=[pltpu.VMEM((nb, hp + ts, d_pool), F32),
                        pltpu.VMEM((nb, hc + ts, d_rnn), F32),
                        pltpu.VMEM((2, nb, pad + ts, d_rnn), F32),
                        pltpu.VMEM((2, nb, pad + ts, d_rnn), F32),
                        pltpu.VMEM((nb * ts, d_pool + d_rnn), BF16),
                        pltpu.VMEM((nb, 1, d_rnn), F32)],
        compiler_params=pltpu.CompilerParams(
            dimension_semantics=("arbitrary", "arbitrary"), vmem_limit_bytes=VMEM_LIMIT_BYTES),
        name="even_mixer",
    )(x, pool0, rconv0, h0, *weights)
    return y, pool, rconv, h.reshape(bsz, d_rnn)


def _odd_kernel(nb, ts,
                x_ref, conv0_ref, w_glu_ref, dw_w_ref, dw_b_ref, cg_ref, cbias_ref, w_pw_ref,
                lng_ref, lnb_ref,
                y_ref, conv_out_ref,
                ge_ref, c_ref):
    t = pl.program_id(1)
    d_model = x_ref.shape[-1]
    d_conv = conv0_ref.shape[-1]
    p_conv = conv0_ref.shape[1]
    kw = dw_w_ref.shape[0]
    hg = ge_ref.shape[1] - ts
    m = nb * ts

    @pl.when(t == 0)
    def _():
        ge_ref[:, hg - p_conv:hg, :] = conv0_ref[...]

    x = x_ref[...].reshape(m, d_model)
    z = _mm(x.astype(BF16), w_glu_ref[...])
    g = z[:, :d_conv] * _sigmoid(z[:, d_conv:])
    ge_ref[:, hg:hg + ts, :] = g.reshape(nb, ts, d_conv)

    base = hg - p_conv
    rb = min(ts, CONV_ROW_BLOCK)

    def conv_lanes(c, carry):
        ls = pl.ds(pl.multiple_of(c * LANES, LANES), LANES)
        for r0 in range(0, ts, rb):
            acc = jnp.broadcast_to(dw_b_ref[:, ls], (nb, rb, LANES))
            for k in range(kw):
                acc = acc + ge_ref[:, base + k + r0:base + k + r0 + rb, ls] * dw_w_ref[k:k + 1, ls]
            c_ref[:, r0:r0 + rb, ls] = acc
        return carry

    lax.fori_loop(0, d_conv // LANES, conv_lanes, 0)

    c = _layer_norm(c_ref[...].reshape(m, d_conv), cg_ref[...], cbias_ref[...])
    c = c * _sigmoid(c)
    out = _mm(c.astype(BF16), w_pw_ref[...])
    y = _layer_norm(ALPHA * x + out, lng_ref[...], lnb_ref[...])
    y_ref[...] = y.reshape(nb, ts, d_model)

    conv_out_ref[...] = ge_ref[:, ts + hg - p_conv:ts + hg, :]
    ge_ref[:, 0:hg, :] = ge_ref[:, ts:ts + hg, :]


def _odd_mixer(x, conv0, w):
    bsz, s, d_model = x.shape
    nb, ts = _seq_tiling(bsz, s)
    d_conv = conv0.shape[-1]
    p_conv = conv0.shape[1]
    hg = 4 * SUBLANES
    assert p_conv <= hg
    weights = (w['w_glu'], w['dw_w'], w['dw_b'], w['cln_g'], w['cln_b'], w['w_pw'], w['ln_g'], w['ln_b'])
    seq3 = lambda b, t: (b, 0, 0)
    y, conv = pl.pallas_call(
        functools.partial(_odd_kernel, nb, ts),
        grid=(bsz // nb, s // ts),
        in_specs=[pl.BlockSpec((nb, ts, d_model), lambda b, t: (b, t, 0)),
                  pl.BlockSpec((nb, p_conv, d_conv), seq3)] + [_full_spec(a) for a in weights],
        out_specs=[pl.BlockSpec((nb, ts, d_model), lambda b, t: (b, t, 0)),
                   pl.BlockSpec((nb, p_conv, d_conv), seq3)],
        out_shape=[jax.ShapeDtypeStruct((bsz, s, d_model), F32),
                   jax.ShapeDtypeStruct((bsz, p_conv, d_conv), F32)],
        scratch_shapes=[pltpu.VMEM((nb, hg + ts, d_conv), F32),
                        pltpu.VMEM((nb, ts, d_conv), F32)],
        compiler_params=pltpu.CompilerParams(
            dimension_semantics=("arbitrary", "arbitrary"), vmem_limit_bytes=VMEM_LIMIT_BYTES),
        name="odd_mixer",
    )(x, conv0, *weights)
    return y, conv


def _top_k_ranks(s, s_ref, rk_ref, v_ref, side):
    nkeys, tn = s.shape
    s_ref[...] = s
    rk_ref[...] = jnp.full((nkeys, tn), float(PEER_TOPK), F32)

    def body(r, carry):
        kio = lax.broadcasted_iota(jnp.int32, (nkeys, tn), 0).astype(F32)
        sc = s_ref[...]
        mx = jnp.max(sc, axis=0, keepdims=True)
        first = jnp.min(jnp.where(sc == mx, kio, float(nkeys)), axis=0, keepdims=True)
        hit = kio == first
        rk_ref[...] = jnp.where(hit, r.astype(F32), rk_ref[...])
        s_ref[...] = jnp.where(hit, -jnp.inf, sc)
        v_ref[side, pl.ds(r, 1), :] = mx
        return carry

    lax.fori_loop(0, PEER_TOPK, body, 0)
    return rk_ref[...], v_ref[side]


def _doubled_bf16_words(x):
    bits = lax.bitcast_convert_type(x.astype(BF16).astype(F32), jnp.int32)
    return bits | lax.shift_right_logical(bits, jnp.full(bits.shape, 16, jnp.int32))


def _peer_prep_kernel(x_ref, wqt_ref, k1_ref, k2_ref,
                      cnt1_ref, e1_ref, rank2_ref, e2_ref,
                      qt_ref, s_ref, rk_ref, v_ref):
    h = pl.program_id(1)
    nkeys, dh = k1_ref.shape
    tn = x_ref.shape[0]
    k = PEER_TOPK

    @pl.when(h == 0)
    def _():
        qt_ref[...] = _mm_nt(wqt_ref[...], x_ref[...].astype(BF16))

    row0 = pl.multiple_of(h * (2 * dh), 2 * dh)
    s1 = _mm(k1_ref[...].astype(BF16), qt_ref[pl.ds(row0, dh), :].astype(BF16))
    s2 = _mm(k2_ref[...].astype(BF16), qt_ref[pl.ds(row0 + dh, dh), :].astype(BF16))

    rank1, v1 = _top_k_ranks(s1, s_ref, rk_ref, v_ref, 0)
    rank2, v2 = _top_k_ranks(s2, s_ref, rk_ref, v_ref, 1)

    iio = lax.broadcasted_iota(jnp.int32, (k, tn), 0).astype(F32)

    def v2_at(cnt):
        g = jnp.full((k, tn), -jnp.inf, F32)
        for j in range(k):
            g = jnp.where(cnt == float(j), v2[j:j + 1, :], g)
        return g

    def walk(_, carry):
        front, cnt = carry
        mx = jnp.max(front, axis=0, keepdims=True)
        first = jnp.min(jnp.where(front == mx, iio, float(k)), axis=0, keepdims=True)
        hit = iio == first
        cnt = cnt + jnp.where(hit, 1.0, 0.0)
        front = jnp.where(hit, v1 + v2_at(cnt), front)
        return front, cnt

    _, cnt = lax.fori_loop(0, k, walk, (v1 + v2[0:1, :], jnp.zeros((k, tn), F32)))

    e1s = jnp.exp(v1 - v1[0:1, :])
    e2s = jnp.exp(v2 - v2[0:1, :])
    prefix = jnp.zeros((1, tn), F32)
    row_mass = jnp.zeros((k, tn), F32)
    for c in range(1, k + 1):
        prefix = prefix + e2s[c - 1:c, :]
        row_mass = jnp.where(cnt == float(c), prefix, row_mass)
    inv_z = 1.0 / jnp.sum(e1s * row_mass, axis=0, keepdims=True)

    cnt_key = jnp.zeros((nkeys, tn), F32)
    for r in range(k):
        cnt_key = jnp.where(rank1 == float(r), cnt[r:r + 1, :], cnt_key)
    e1 = jnp.exp(s1 - v1[0:1, :]) * inv_z
    e2 = jnp.exp(s2 - v2[0:1, :])

    cnt1_ref[0] = _doubled_bf16_words(cnt_key)
    e1_ref[0] = _doubled_bf16_words(e1)
    rank2_ref[0] = rank2.astype(BF16)
    e2_ref[0] = e2.astype(BF16)


def _peer_prep(x2, wqt, k1, k2):
    ntok, d_model = x2.shape
    nkeys, dh = k1.shape
    heads = wqt.shape[0] // (2 * dh)
    tn = min(PEER_TN, ntok)
    assert ntok % tn == 0 and nkeys == LANES
    table = lambda dt: jax.ShapeDtypeStruct((heads, nkeys, ntok), dt)
    tspec = pl.BlockSpec((1, nkeys, tn), lambda i, h: (h, 0, i))
    return pl.pallas_call(
        _peer_prep_kernel,
        grid=(ntok // tn, heads),
        in_specs=[pl.BlockSpec((tn, d_model), lambda i, h: (i, 0)),
                  pl.BlockSpec(wqt.shape, lambda i, h: (0, 0)),
                  pl.BlockSpec(k1.shape, lambda i, h: (0, 0)),
                  pl.BlockSpec(k2.shape, lambda i, h: (0, 0))],
        out_specs=[tspec, tspec, tspec, tspec],
        out_shape=[table(jnp.int32), table(jnp.int32), table(BF16), table(BF16)],
        scratch_shapes=[pltpu.VMEM((wqt.shape[0], tn), F32),
                        pltpu.VMEM((nkeys, tn), F32),
                        pltpu.VMEM((nkeys, tn), F32),
                        pltpu.VMEM((2, PEER_TOPK, tn), F32)],
        compiler_params=pltpu.CompilerParams(
            dimension_semantics=("arbitrary", "arbitrary"), vmem_limit_bytes=VMEM_LIMIT_BYTES),
        name="peer_prep",
    )(x2, wqt, k1, k2)


def _peer_main_kernel(x_ref, u_ref, vt_ref, cnt1_ref, e1_ref, rank2_ref, e2_ref, lng_ref, lnb_ref,
                      y_ref,
                      xb_ref, at_ref, ht_ref, acc_ref):
    j = pl.program_id(1)
    heads, nkeys, tn = rank2_ref.shape
    te = u_ref.shape[0]

    @pl.when(j == 0)
    def _():
        xb_ref[...] = x_ref[...].astype(BF16)
        acc_ref[...] = jnp.zeros(acc_ref.shape, F32)

    at_ref[...] = _mm_nt(u_ref[...], xb_ref[...])

    for al in range(te // nkeys):
        for lt in range(tn // LANES):
            ls = slice(lt * LANES, (lt + 1) * LANES)

            def row_tile(ref, hh):
                words = jnp.broadcast_to(ref[hh, al:al + 1, ls], (SUBLANES, LANES))
                return pltpu.bitcast(words, BF16)

            cnts = [row_tile(cnt1_ref, hh) for hh in range(heads)]
            e1s = [row_tile(e1_ref, hh) for hh in range(heads)]
            for bg in range(nkeys // BF16_ROWS):
                rs = slice(bg * BF16_ROWS, (bg + 1) * BF16_ROWS)
                gate = None
                for hh in range(heads):
                    term = jnp.where(rank2_ref[hh, rs, ls] < cnts[hh],
                                     e2_ref[hh, rs, ls] * e1s[hh], jnp.zeros((), BF16))
                    gate = term if gate is None else gate + term
                es = slice(al * nkeys + bg * BF16_ROWS, al * nkeys + (bg + 1) * BF16_ROWS)
                act = _gelu_tanh(at_ref[es, ls])
                ht_ref[es, ls] = (act * gate.astype(F32)).astype(BF16)

    acc_ref[...] += _mm(vt_ref[...], ht_ref[...])

    @pl.when(j == pl.num_programs(1) - 1)
    def _():
        f = acc_ref[...].T
        y_ref[...] = _layer_norm(ALPHA * x_ref[...] + f, lng_ref[...], lnb_ref[...])


def _peer_main(x2, u_bf, vt_bf, tables, ln_g, ln_b):
    ntok, d_model = x2.shape
    cnt1, e1, rank2, e2 = tables
    heads, nkeys, _ = rank2.shape
    nexp = u_bf.shape[0]
    tn = min(PEER_TN, ntok)
    te = PEER_TE
    assert nexp % te == 0 and te % nkeys == 0 and (te // nkeys) % SUBLANES == 0
    a_rows = te // nkeys
    row_spec = pl.BlockSpec((heads, a_rows, tn), lambda i, j: (0, j, i))
    col_spec = pl.BlockSpec((heads, nkeys, tn), lambda i, j: (0, 0, i))
    vec_spec = pl.BlockSpec((1, d_model), lambda i, j: (0, 0))
    return pl.pallas_call(
        _peer_main_kernel,
        grid=(ntok // tn, nexp // te),
        in_specs=[pl.BlockSpec((tn, d_model), lambda i, j: (i, 0)),
                  pl.BlockSpec((te, d_model), lambda i, j: (j, 0)),
                  pl.BlockSpec((d_model, te), lambda i, j: (0, j)),
                  row_spec, row_spec, col_spec, col_spec, vec_spec, vec_spec],
        out_specs=pl.BlockSpec((tn, d_model), lambda i, j: (i, 0)),
        out_shape=jax.ShapeDtypeStruct((ntok, d_model), F32),
        scratch_shapes=[pltpu.VMEM((tn, d_model), BF16),
                        pltpu.VMEM((te, tn), F32),
                        pltpu.VMEM((te, tn), BF16),
                        pltpu.VMEM((d_model, tn), F32)],
        compiler_params=pltpu.CompilerParams(
            dimension_semantics=("arbitrary", "arbitrary"), vmem_limit_bytes=VMEM_LIMIT_BYTES),
        name="peer_main",
    )(x2, u_bf, vt_bf, cnt1, e1, rank2, e2, ln_g, ln_b)


def _peer(x, w):
    bsz, s, d_model = x.shape
    x2 = x.reshape(bsz * s, d_model)
    tables = _peer_prep(x2, w['wqt'], w['k1'], w['k2'])
    y = _peer_main(x2, w['u'], w['vt'], tables, w['ln_g'], w['ln_b'])
    return y.reshape(bsz, s, d_model)


def _block_diag(w):
    heads, di, do = w.shape
    eye = jnp.eye(heads, dtype=w.dtype)
    return (eye[:, None, :, None] * w[:, :, None, :]).reshape(heads * di, heads * do)


def _row(v):
    return v.reshape(1, -1).astype(F32)


def _trunk(x, st_pool, st_rconv, st_h, st_conv, start, even_w, odd_w, peer_w):
    pools, rconvs, hs, convs = [], [], [], []
    for layer in range(DEPTH):
        j = layer // 2
        if layer % 2 == 0:
            x, pb, rb, hl = _even_mixer(x, st_pool[j], st_rconv[j], st_h[j], start, even_w[j])
            pools.append(pb)
            rconvs.append(rb)
            hs.append(hl)
        else:
            x, cb = _odd_mixer(x, st_conv[j], odd_w[j])
            convs.append(cb)
        x = _peer(x, peer_w[layer])
    return x, jnp.stack(pools), jnp.stack(rconvs), jnp.stack(hs), jnp.stack(convs)


def kernel(x_prompt, x_sample, state_pool, state_rnn_conv, state_rnn_h, state_conv, w_in_even, pool_w, pool_scale, rnn_conv_w, rnn_conv_b, rg_wa, rg_ba, rg_wx, rg_bx, rg_lambda, w_out_even, w_glu, dw_w, dw_b, conv_ln_g, conv_ln_b, w_pw, peer_wq, peer_k1, peer_k2, peer_u, peer_v, ln_mix_g, ln_mix_b, ln_ffn_g, ln_ffn_b):
    n_even, n_odd = w_in_even.shape[0], w_glu.shape[0]
    even_w = [dict(w_in=w_in_even[j].astype(BF16), pool_w=pool_w[j].astype(BF16),
                   pool_scale=_row(pool_scale[j]), conv_w=rnn_conv_w[j], conv_b=_row(rnn_conv_b[j]),
                   wa=_block_diag(rg_wa[j]).astype(BF16), ba=_row(rg_ba[j]),
                   wx=_block_diag(rg_wx[j]).astype(BF16), bx=_row(rg_bx[j]),
                   lam=_row(rg_lambda[j]), w_out=w_out_even[j].astype(BF16),
                   ln_g=_row(ln_mix_g[2 * j]), ln_b=_row(ln_mix_b[2 * j]))
              for j in range(n_even)]
    odd_w = [dict(w_glu=w_glu[j].astype(BF16), dw_w=dw_w[j], dw_b=_row(dw_b[j]),
                  cln_g=_row(conv_ln_g[j]), cln_b=_row(conv_ln_b[j]), w_pw=w_pw[j].astype(BF16),
                  ln_g=_row(ln_mix_g[2 * j + 1]), ln_b=_row(ln_mix_b[2 * j + 1]))
             for j in range(n_odd)]
    peer_w = [dict(wqt=peer_wq[l].T.astype(BF16), k1=peer_k1[l], k2=peer_k2[l],
                   u=peer_u[l].astype(BF16), vt=peer_v[l].T.astype(BF16),
                   ln_g=_row(ln_ffn_g[l]), ln_b=_row(ln_ffn_b[l]))
              for l in range(DEPTH)]

    bp = x_prompt.shape[0]
    dt = x_prompt.dtype
    zeros_like_state = lambda st: jnp.zeros((st.shape[0], bp) + st.shape[2:], dt)
    y_p, pool_p, rconv_p, h_p, conv_p = _trunk(
        x_prompt, zeros_like_state(state_pool), zeros_like_state(state_rnn_conv),
        zeros_like_state(state_rnn_h), zeros_like_state(state_conv), 0, even_w, odd_w, peer_w)
    y_s, pool_s, rconv_s, h_s, conv_s = _trunk(
        x_sample, state_pool, state_rnn_conv, state_rnn_h, state_conv, PAST_LEN, even_w, odd_w, peer_w)
    return (y_p, y_s, pool_p, rconv_p, h_p, conv_p, pool_s, rconv_s, h_s, conv_s)
```

```python
import functools

import jax
import jax.numpy as jnp
from jax import lax
from jax.experimental import pallas as pl
from jax.experimental.pallas import tpu as pltpu

F32 = jnp.float32
BF16 = jnp.bfloat16

DEPTH = 2
PAST_LEN = 16384
POOL_WINDOWS = (2, 4, 8, 16)
RG_C = 8.0
PEER_TOPK = 16
LN_EPS = 1e-5
ALPHA = (2.0 * DEPTH) ** 0.25

LANES = 128
SUBLANES = 8
BF16_ROWS = 16
VMEM_LIMIT_BYTES = 56 * 1024 * 1024

PROMPT_TS = 512
SAMPLE_NB = 32
CONV_ROW_BLOCK = 128
PEER_TN = 512
PEER_TE = 1024


def _layer_norm(v, g, b):
    mu = jnp.mean(v, axis=-1, keepdims=True)
    d = v - mu
    var = jnp.mean(d * d, axis=-1, keepdims=True)
    return d * lax.rsqrt(var + LN_EPS) * g + b


GELU_C1 = 0.7978845608028654
GELU_C2 = GELU_C1 * 0.044715


def _gelu_tanh(x):
    return 0.5 * x * (1.0 + jnp.tanh(x * (GELU_C1 + GELU_C2 * (x * x))))


def _sigmoid(x):
    return 1.0 / (1.0 + jnp.exp(-x))


def _mm(a, b):
    return jnp.dot(a, b, preferred_element_type=F32)


def _mm_nt(a, b):
    return lax.dot_general(a, b, (((1,), (1,)), ((), ())), preferred_element_type=F32)


def _even_kernel(start, nb, ts, pad,
                 x_ref, pool0_ref, rconv0_ref, h0_ref, w_in_ref, pool_w_ref, pool_scale_ref,
                 cw_ref, cb_ref, wa_ref, ba_ref, wx_ref, bx_ref, lam_ref, w_out_ref,
                 lng_ref, lnb_ref,
                 y_ref, pool_out_ref, rconv_out_ref, h_out_ref,
                 pa_ref, cv_ref, sa_ref, su_ref, cat_ref, hc_ref):
    t = pl.program_id(1)
    d_model = x_ref.shape[-1]
    d_pool = pool0_ref.shape[-1]
    d_rnn = rconv0_ref.shape[-1]
    p_pool = pool0_ref.shape[1]
    p_conv = rconv0_ref.shape[1]
    hp = pa_ref.shape[1] - ts
    hc = cv_ref.shape[1] - ts
    m = nb * ts
    group = d_pool // len(POOL_WINDOWS)

    @pl.when(t == 0)
    def _():
        pa_ref[:, hp - p_pool:hp, :] = pool0_ref[...]
        cv_ref[:, hc - p_conv:hc, :] = rconv0_ref[...]
        hc_ref[...] = h0_ref[...]

    x = x_ref[...].reshape(m, d_model)
    z = _mm(x.astype(BF16), w_in_ref[...])
    pa_ref[:, hp:hp + ts, :] = z[:, :d_pool].reshape(nb, ts, d_pool)
    cv_ref[:, hc:hc + ts, :] = z[:, d_pool:d_pool + d_rnn].reshape(nb, ts, d_rnn)
    xg = z[:, d_pool + d_rnn:]

    pos = start + t * ts + lax.broadcasted_iota(jnp.int32, (nb, ts, group), 1)
    for g, w in enumerate(POOL_WINDOWS):
        cs = slice(g * group, (g + 1) * group)
        tok = pa_ref[:, hp:hp + ts, cs]
        win = tok
        for d in range(1, w):
            win = win + pa_ref[:, hp - d:hp - d + ts, cs]
        cnt = jnp.minimum(pos + 1, w).astype(F32)
        pooled = (win / cnt - tok).reshape(m, group)
        ya = _mm(pooled.astype(BF16), pool_w_ref[g]) * pool_scale_ref[:, cs]
        cat_ref[:, cs] = ya.astype(BF16)

    kw = cw_ref.shape[0]
    xc = cv_ref[:, hc - p_conv:hc - p_conv + ts, :] * cw_ref[0:1, :]
    for k in range(1, kw):
        xc = xc + cv_ref[:, hc - p_conv + k:hc - p_conv + k + ts, :] * cw_ref[k:k + 1, :]
    xc = (xc + cb_ref[...]).reshape(m, d_rnn)
    xcb = xc.astype(BF16)
    r = _sigmoid(_mm(xcb, wa_ref[...]) + ba_ref[...])
    i = _sigmoid(_mm(xcb, wx_ref[...]) + bx_ref[...])
    nl = -lam_ref[...]
    softplus = jnp.maximum(nl, 0.0) + jnp.log(1.0 + jnp.exp(-jnp.abs(nl)))
    log_a = (-RG_C) * r * softplus
    a = jnp.exp(log_a)
    u = jnp.sqrt(jnp.tanh(-log_a) * (a * a + 1.0)) * (i * xc)

    sa_ref[:, :, :pad, :] = jnp.ones((2, nb, pad, d_rnn), F32)
    su_ref[:, :, :pad, :] = jnp.zeros((2, nb, pad, d_rnn), F32)
    sa_ref[0, :, pad:, :] = a.reshape(nb, ts, d_rnn)
    su_ref[0, :, pad:, :] = u.reshape(nb, ts, d_rnn)
    src = 0
    k = 1
    while k < ts:
        dst = 1 - src
        a_cur = sa_ref[src, :, pad:, :]
        su_ref[dst, :, pad:, :] = su_ref[src, :, pad:, :] + a_cur * su_ref[src, :, pad - k:pad - k + ts, :]
        sa_ref[dst, :, pad:, :] = a_cur * sa_ref[src, :, pad - k:pad - k + ts, :]
        src = dst
        k *= 2
    h = su_ref[src, :, pad:, :] + sa_ref[src, :, pad:, :] * hc_ref[...]
    hc_ref[...] = h[:, ts - 1:ts, :]
    h_out_ref[...] = h[:, ts - 1:ts, :]

    cat_ref[:, d_pool:] = (h.reshape(m, d_rnn) * _gelu_tanh(xg)).astype(BF16)
    out = _mm(cat_ref[...], w_out_ref[...])
    y = _layer_norm(ALPHA * x + out, lng_ref[...], lnb_ref[...])
    y_ref[...] = y.reshape(nb, ts, d_model)

    pool_out_ref[...] = pa_ref[:, ts + hp - p_pool:ts + hp, :]
    rconv_out_ref[...] = cv_ref[:, ts + hc - p_conv:ts + hc, :]
    pa_ref[:, 0:hp, :] = pa_ref[:, ts:ts + hp, :]
    cv_ref[:, 0:hc, :] = cv_ref[:, ts:ts + hc, :]


def _seq_tiling(bsz, s):
    if s > PROMPT_TS:
        assert s % PROMPT_TS == 0
        return 1, PROMPT_TS
    nb = min(bsz, SAMPLE_NB)
    assert bsz % nb == 0
    return nb, s


def _full_spec(arr):
    nd = arr.ndim
    return pl.BlockSpec(arr.shape, lambda b, t, _n=nd: (0,) * _n)


def _even_mixer(x, pool0, rconv0, h0, start, w):
    bsz, s, d_model = x.shape
    nb, ts = _seq_tiling(bsz, s)
    d_pool = pool0.shape[-1]
    d_rnn = rconv0.shape[-1]
    p_pool, p_conv = pool0.shape[1], rconv0.shape[1]
    hp, hc = 2 * SUBLANES, SUBLANES
    pad = max(SUBLANES, ts // 2)
    h0 = h0.reshape(bsz, 1, d_rnn)
    weights = (w['w_in'], w['pool_w'], w['pool_scale'], w['conv_w'], w['conv_b'], w['wa'], w['ba'],
               w['wx'], w['bx'], w['lam'], w['w_out'], w['ln_g'], w['ln_b'])
    seq3 = lambda b, t: (b, 0, 0)
    y, pool, rconv, h = pl.pallas_call(
        functools.partial(_even_kernel, start, nb, ts, pad),
        grid=(bsz // nb, s // ts),
        in_specs=[pl.BlockSpec((nb, ts, d_model), lambda b, t: (b, t, 0)),
                  pl.BlockSpec((nb, p_pool, d_pool), seq3),
                  pl.BlockSpec((nb, p_conv, d_rnn), seq3),
                  pl.BlockSpec((nb, 1, d_rnn), seq3)] + [_full_spec(a) for a in weights],
        out_specs=[pl.BlockSpec((nb, ts, d_model), lambda b, t: (b, t, 0)),
                   pl.BlockSpec((nb, p_pool, d_pool), seq3),
                   pl.BlockSpec((nb, p_conv, d_rnn), seq3),
                   pl.BlockSpec((nb, 1, d_rnn), seq3)],
        out_shape=[jax.ShapeDtypeStruct((bsz, s, d_model), F32),
                   jax.ShapeDtypeStruct((bsz, p_pool, d_pool), F32),
                   jax.ShapeDtypeStruct((bsz, p_conv, d_rnn), F32),
                   jax.ShapeDtypeStruct((bsz, 1, d_rnn), F32)],
        scratch_shapes=[pltpu.VMEM((nb, hp + ts, d_pool), F32),
                        pltpu.VMEM((nb, hc + ts, d_rnn), F32),
                        pltpu.VMEM((2, nb, pad + ts, d_rnn), F32),
                        pltpu.VMEM((2, nb, pad + ts, d_rnn), F32),
                        pltpu.VMEM((nb * ts, d_pool + d_rnn), BF16),
                        pltpu.VMEM((nb, 1, d_rnn), F32)],
        compiler_params=pltpu.CompilerParams(
            dimension_semantics=("arbitrary", "arbitrary"), vmem_limit_bytes=VMEM_LIMIT_BYTES),
        name="even_mixer",
    )(x, pool0, rconv0, h0, *weights)
    return y, pool, rconv, h.reshape(bsz, d_rnn)


def _odd_kernel(nb, ts,
                x_ref, conv0_ref, w_glu_ref, dw_w_ref, dw_b_ref, cg_ref, cbias_ref, w_pw_ref,
                lng_ref, lnb_ref,
                y_ref, conv_out_ref,
                ge_ref, c_ref):
    t = pl.program_id(1)
    d_model = x_ref.shape[-1]
    d_conv = conv0_ref.shape[-1]
    p_conv = conv0_ref.shape[1]
    kw = dw_w_ref.shape[0]
    hg = ge_ref.shape[1] - ts
    m = nb * ts

    @pl.when(t == 0)
    def _():
        ge_ref[:, hg - p_conv:hg, :] = conv0_ref[...]

    x = x_ref[...].reshape(m, d_model)
    z = _mm(x.astype(BF16), w_glu_ref[...])
    g = z[:, :d_conv] * _sigmoid(z[:, d_conv:])
    ge_ref[:, hg:hg + ts, :] = g.reshape(nb, ts, d_conv)

    base = hg - p_conv
    rb = min(ts, CONV_ROW_BLOCK)

    def conv_lanes(c, carry):
        ls = pl.ds(pl.multiple_of(c * LANES, LANES), LANES)
        for r0 in range(0, ts, rb):
            acc = jnp.broadcast_to(dw_b_ref[:, ls], (nb, rb, LANES))
            for k in range(kw):
                acc = acc + ge_ref[:, base + k + r0:base + k + r0 + rb, ls] * dw_w_ref[k:k + 1, ls]
            c_ref[:, r0:r0 + rb, ls] = acc
        return carry

    lax.fori_loop(0, d_conv // LANES, conv_lanes, 0)

    c = _layer_norm(c_ref[...].reshape(m, d_conv), cg_ref[...], cbias_ref[...])
    c = c * _sigmoid(c)
    out = _mm(c.astype(BF16), w_pw_ref[...])
    y = _layer_norm(ALPHA * x + out, lng_ref[...], lnb_ref[...])
    y_ref[...] = y.reshape(nb, ts, d_model)

    conv_out_ref[...] = ge_ref[:, ts + hg - p_conv:ts + hg, :]
    ge_ref[:, 0:hg, :] = ge_ref[:, ts:ts + hg, :]


def _odd_mixer(x, conv0, w):
    bsz, s, d_model = x.shape
    nb, ts = _seq_tiling(bsz, s)
    d_conv = conv0.shape[-1]
    p_conv = conv0.shape[1]
    hg = 4 * SUBLANES
    assert p_conv <= hg
    weights = (w['w_glu'], w['dw_w'], w['dw_b'], w['cln_g'], w['cln_b'], w['w_pw'], w['ln_g'], w['ln_b'])
    seq3 = lambda b, t: (b, 0, 0)
    y, conv = pl.pallas_call(
        functools.partial(_odd_kernel, nb, ts),
        grid=(bsz // nb, s // ts),
        in_specs=[pl.BlockSpec((nb, ts, d_model), lambda b, t: (b, t, 0)),
                  pl.BlockSpec((nb, p_conv, d_conv), seq3)] + [_full_spec(a) for a in weights],
        out_specs=[pl.BlockSpec((nb, ts, d_model), lambda b, t: (b, t, 0)),
                   pl.BlockSpec((nb, p_conv, d_conv), seq3)],
        out_shape=[jax.ShapeDtypeStruct((bsz, s, d_model), F32),
                   jax.ShapeDtypeStruct((bsz, p_conv, d_conv), F32)],
        scratch_shapes=[pltpu.VMEM((nb, hg + ts, d_conv), F32),
                        pltpu.VMEM((nb, ts, d_conv), F32)],
        compiler_params=pltpu.CompilerParams(
            dimension_semantics=("arbitrary", "arbitrary"), vmem_limit_bytes=VMEM_LIMIT_BYTES),
        name="odd_mixer",
    )(x, conv0, *weights)
    return y, conv


def _top_k_ranks(s, s_ref, rk_ref, v_ref, side):
    nkeys, tn = s.shape
    s_ref[...] = s
    rk_ref[...] = jnp.full((nkeys, tn), float(PEER_TOPK), F32)

    def body(r, carry):
        kio = lax.broadcasted_iota(jnp.int32, (nkeys, tn), 0).astype(F32)
        sc = s_ref[...]
        mx = jnp.max(sc, axis=0, keepdims=True)
        first = jnp.min(jnp.where(sc == mx, kio, float(nkeys)), axis=0, keepdims=True)
        hit = kio == first
        rk_ref[...] = jnp.where(hit, r.astype(F32), rk_ref[...])
        s_ref[...] = jnp.where(hit, -jnp.inf, sc)
        v_ref[side, pl.ds(r, 1), :] = mx
        return carry

    lax.fori_loop(0, PEER_TOPK, body, 0)
    return rk_ref[...], v_ref[side]


def _doubled_bf16_words(x):
    bits = lax.bitcast_convert_type(x.astype(BF16).astype(F32), jnp.int32)
    return bits | lax.shift_right_logical(bits, jnp.full(bits.shape, 16, jnp.int32))


def _peer_prep_kernel(x_ref, wqt_ref, k1_ref, k2_ref,
                      cnt1_ref, e1_ref, rank2_ref, e2_ref,
                      qt_ref, s_ref, rk_ref, v_ref):
    h = pl.program_id(1)
    nkeys, dh = k1_ref.shape
    tn = x_ref.shape[0]
    k = PEER_TOPK

    @pl.when(h == 0)
    def _():
        qt_ref[...] = _mm_nt(wqt_ref[...], x_ref[...].astype(BF16))

    row0 = pl.multiple_of(h * (2 * dh), 2 * dh)
    s1 = _mm(k1_ref[...].astype(BF16), qt_ref[pl.ds(row0, dh), :].astype(BF16))
    s2 = _mm(k2_ref[...].astype(BF16), qt_ref[pl.ds(row0 + dh, dh), :].astype(BF16))

    rank1, v1 = _top_k_ranks(s1, s_ref, rk_ref, v_ref, 0)
    rank2, v2 = _top_k_ranks(s2, s_ref, rk_ref, v_ref, 1)

    iio = lax.broadcasted_iota(jnp.int32, (k, tn), 0).astype(F32)

    def v2_at(cnt):
        g = jnp.full((k, tn), -jnp.inf, F32)
        for j in range(k):
            g = jnp.where(cnt == float(j), v2[j:j + 1, :], g)
        return g

    def walk(_, carry):
        front, cnt = carry
        mx = jnp.max(front, axis=0, keepdims=True)
        first = jnp.min(jnp.where(front == mx, iio, float(k)), axis=0, keepdims=True)
        hit = iio == first
        cnt = cnt + jnp.where(hit, 1.0, 0.0)
        front = jnp.where(hit, v1 + v2_at(cnt), front)
        return front, cnt

    _, cnt = lax.fori_loop(0, k, walk, (v1 + v2[0:1, :], jnp.zeros((k, tn), F32)))

    e1s = jnp.exp(v1 - v1[0:1, :])
    e2s = jnp.exp(v2 - v2[0:1, :])
    prefix = jnp.zeros((1, tn), F32)
    row_mass = jnp.zeros((k, tn), F32)
    for c in range(1, k + 1):
        prefix = prefix + e2s[c - 1:c, :]
        row_mass = jnp.where(cnt == float(c), prefix, row_mass)
    inv_z = 1.0 / jnp.sum(e1s * row_mass, axis=0, keepdims=True)

    cnt_key = jnp.zeros((nkeys, tn), F32)
    for r in range(k):
        cnt_key = jnp.where(rank1 == float(r), cnt[r:r + 1, :], cnt_key)
    e1 = jnp.exp(s1 - v1[0:1, :]) * (0.5 * inv_z)
    e2 = jnp.exp(s2 - v2[0:1, :])

    cnt1_ref[0] = _doubled_bf16_words(cnt_key)
    e1_ref[0] = _doubled_bf16_words(e1)
    rank2_ref[0] = pltpu.bitcast(rank2.astype(BF16), jnp.int32)
    e2_ref[0] = pltpu.bitcast(e2.astype(BF16), jnp.int32)


def _peer_prep(x2, wqt, k1, k2):
    ntok, d_model = x2.shape
    nkeys, dh = k1.shape
    heads = wqt.shape[0] // (2 * dh)
    tn = min(PEER_TN, ntok)
    assert ntok % tn == 0 and tn % LANES == 0 and nkeys == LANES
    pair_rows = nkeys // 2
    table = lambda rows: jax.ShapeDtypeStruct((heads, rows, ntok), jnp.int32)
    tspec = lambda rows: pl.BlockSpec((1, rows, tn), lambda i, h: (h, 0, i))
    return pl.pallas_call(
        _peer_prep_kernel,
        grid=(ntok // tn, heads),
        in_specs=[pl.BlockSpec((tn, d_model), lambda i, h: (i, 0)),
                  pl.BlockSpec(wqt.shape, lambda i, h: (0, 0)),
                  pl.BlockSpec(k1.shape, lambda i, h: (0, 0)),
                  pl.BlockSpec(k2.shape, lambda i, h: (0, 0))],
        out_specs=[tspec(nkeys), tspec(nkeys), tspec(pair_rows), tspec(pair_rows)],
        out_shape=[table(nkeys), table(nkeys), table(pair_rows), table(pair_rows)],
        scratch_shapes=[pltpu.VMEM((wqt.shape[0], tn), F32),
                        pltpu.VMEM((nkeys, tn), F32),
                        pltpu.VMEM((nkeys, tn), F32),
                        pltpu.VMEM((2, PEER_TOPK, tn), F32)],
        compiler_params=pltpu.CompilerParams(
            dimension_semantics=("arbitrary", "arbitrary"), vmem_limit_bytes=VMEM_LIMIT_BYTES),
        name="peer_prep",
    )(x2, wqt, k1, k2)


def _gate_and_activate(at_ref, ht_ref, cnt_ref, e1_ref, rank2_ref, e2_ref):
    heads, pair_rows, tn = rank2_ref.shape
    nkeys = 2 * pair_rows
    n_a = at_ref.shape[0] // nkeys
    n_bg = nkeys // BF16_ROWS
    for lt in range(tn // LANES):
        ls = slice(lt * LANES, (lt + 1) * LANES)

        def row_tile(ref, hh, al):
            words = jnp.broadcast_to(ref[hh, al:al + 1, ls], (SUBLANES, LANES))
            return pltpu.bitcast(words, BF16)

        gates = [[None] * n_bg for _ in range(n_a)]
        for hh in range(heads):
            cnts = [row_tile(cnt_ref, hh, al) for al in range(n_a)]
            e1s = [row_tile(e1_ref, hh, al) for al in range(n_a)]
            for bg in range(n_bg):
                ws = slice(bg * SUBLANES, (bg + 1) * SUBLANES)
                rank2 = pltpu.bitcast(rank2_ref[hh, ws, ls], BF16)
                e2 = pltpu.bitcast(e2_ref[hh, ws, ls], BF16)
                for al in range(n_a):
                    term = jnp.where(rank2 < cnts[al], e2 * e1s[al], jnp.zeros((), BF16))
                    gates[al][bg] = term if hh == 0 else gates[al][bg] + term
        for al in range(n_a):
            for bg in range(n_bg):
                es = slice(al * nkeys + bg * BF16_ROWS, al * nkeys + (bg + 1) * BF16_ROWS)
                a = at_ref[es, ls].astype(BF16)
                act = a * (1.0 + jnp.tanh(a * (GELU_C1 + GELU_C2 * (a * a))))
                ht_ref[es, ls] = act * gates[al][bg]


def _peer_main_kernel(x_ref, u_ref, vt_ref, cnt1_ref, e1_ref, rank2_ref, e2_ref, lng_ref, lnb_ref,
                      y_ref,
                      xb_ref, at0_ref, at1_ref, ht0_ref, ht1_ref, pcnt_ref, pe1_ref, acc_ref):
    j = pl.program_id(1)
    half = at0_ref.shape[0]
    a_half = pcnt_ref.shape[1]

    @pl.when(j == 0)
    def _():
        xb_ref[...] = x_ref[...].astype(BF16)
        acc_ref[...] = jnp.zeros(acc_ref.shape, F32)
        at1_ref[...] = jnp.zeros(at1_ref.shape, F32)
        ht0_ref[...] = jnp.zeros(ht0_ref.shape, BF16)
        ht1_ref[...] = jnp.zeros(ht1_ref.shape, BF16)
        pcnt_ref[...] = jnp.zeros(pcnt_ref.shape, jnp.int32)
        pe1_ref[...] = jnp.zeros(pe1_ref.shape, jnp.int32)

    xb = xb_ref[...]
    hw = half // 2
    f0 = _mm(pltpu.bitcast(vt_ref[:, :half], BF16), ht0_ref[...])
    at0_ref[...] = _mm_nt(pltpu.bitcast(u_ref[:hw, :], BF16), xb)
    _gate_and_activate(at1_ref, ht1_ref, pcnt_ref, pe1_ref, rank2_ref, e2_ref)

    f1 = _mm(pltpu.bitcast(vt_ref[:, half:], BF16), ht1_ref[...])
    at1_ref[...] = _mm_nt(pltpu.bitcast(u_ref[hw:, :], BF16), xb)
    _gate_and_activate(at0_ref, ht0_ref, cnt1_ref, e1_ref, rank2_ref, e2_ref)

    acc_ref[...] += f0 + f1
    pcnt_ref[...] = cnt1_ref[:, a_half:, :]
    pe1_ref[...] = e1_ref[:, a_half:, :]

    @pl.when(j == pl.num_programs(1) - 1)
    def _():
        f = acc_ref[...].T
        y_ref[...] = _layer_norm(ALPHA * x_ref[...] + f, lng_ref[...], lnb_ref[...])


def _peer_main(x2, u_words, vt_words, tables, ln_g, ln_b):
    ntok, d_model = x2.shape
    cnt1, e1, rank2, e2 = tables
    heads, nkeys, _ = cnt1.shape
    nexp = 2 * u_words.shape[0]
    tn = min(PEER_TN, ntok)
    te = PEER_TE
    assert nexp % te == 0 and te % (2 * nkeys) == 0 and (te // nkeys) % SUBLANES == 0
    a_rows = te // nkeys
    nblk = nexp // te
    last = nblk - 1
    row_spec = pl.BlockSpec((heads, a_rows, tn), lambda i, j: (0, jnp.minimum(j, last), i))
    col_spec = pl.BlockSpec((heads, nkeys // 2, tn), lambda i, j: (0, 0, i))
    vec_spec = pl.BlockSpec((1, d_model), lambda i, j: (0, 0))
    return pl.pallas_call(
        _peer_main_kernel,
        grid=(ntok // tn, nblk + 1),
        in_specs=[pl.BlockSpec((tn, d_model), lambda i, j: (i, 0)),
                  pl.BlockSpec((te // 2, d_model), lambda i, j: (jnp.minimum(j, last), 0)),
                  pl.BlockSpec((d_model // 2, te), lambda i, j: (0, jnp.maximum(j - 1, 0))),
                  row_spec, row_spec, col_spec, col_spec, vec_spec, vec_spec],
        out_specs=pl.BlockSpec((tn, d_model), lambda i, j: (i, 0)),
        out_shape=jax.ShapeDtypeStruct((ntok, d_model), F32),
        scratch_shapes=[pltpu.VMEM((tn, d_model), BF16),
                        pltpu.VMEM((te // 2, tn), F32),
                        pltpu.VMEM((te // 2, tn), F32),
                        pltpu.VMEM((te // 2, tn), BF16),
                        pltpu.VMEM((te // 2, tn), BF16),
                        pltpu.VMEM((heads, a_rows // 2, tn), jnp.int32),
                        pltpu.VMEM((heads, a_rows // 2, tn), jnp.int32),
                        pltpu.VMEM((d_model, tn), F32)],
        compiler_params=pltpu.CompilerParams(
            dimension_semantics=("arbitrary", "arbitrary"), vmem_limit_bytes=VMEM_LIMIT_BYTES),
        name="peer_main",
    )(x2, u_words, vt_words, cnt1, e1, rank2, e2, ln_g, ln_b)


def _peer(x, w):
    bsz, s, d_model = x.shape
    x2 = x.reshape(bsz * s, d_model)
    tables = _peer_prep(x2, w['wqt'], w['k1'], w['k2'])
    y = _peer_main(x2, w['u'], w['vt'], tables, w['ln_g'], w['ln_b'])
    return y.reshape(bsz, s, d_model)


def _block_diag(w):
    heads, di, do = w.shape
    eye = jnp.eye(heads, dtype=w.dtype)
    return (eye[:, None, :, None] * w[:, :, None, :]).reshape(heads * di, heads * do)


def _bf16_row_pair_words(w):
    rows, cols = w.shape
    pairs = jnp.swapaxes(w.astype(BF16).reshape(rows // 2, 2, cols), 1, 2)
    return lax.bitcast_convert_type(pairs, jnp.int32)


def _row(v):
    return v.reshape(1, -1).astype(F32)


def _trunk(x, st_pool, st_rconv, st_h, st_conv, start, even_w, odd_w, peer_w):
    pools, rconvs, hs, convs = [], [], [], []
    for layer in range(DEPTH):
        j = layer // 2
        if layer % 2 == 0:
            x, pb, rb, hl = _even_mixer(x, st_pool[j], st_rconv[j], st_h[j], start, even_w[j])
            pools.append(pb)
            rconvs.append(rb)
            hs.append(hl)
        else:
            x, cb = _odd_mixer(x, st_conv[j], odd_w[j])
            convs.append(cb)
        x = _peer(x, peer_w[layer])
    return x, jnp.stack(pools), jnp.stack(rconvs), jnp.stack(hs), jnp.stack(convs)


def kernel(x_prompt, x_sample, state_pool, state_rnn_conv, state_rnn_h, state_conv, w_in_even, pool_w, pool_scale, rnn_conv_w, rnn_conv_b, rg_wa, rg_ba, rg_wx, rg_bx, rg_lambda, w_out_even, w_glu, dw_w, dw_b, conv_ln_g, conv_ln_b, w_pw, peer_wq, peer_k1, peer_k2, peer_u, peer_v, ln_mix_g, ln_mix_b, ln_ffn_g, ln_ffn_b):
    n_even, n_odd = w_in_even.shape[0], w_glu.shape[0]
    even_w = [dict(w_in=w_in_even[j].astype(BF16), pool_w=pool_w[j].astype(BF16),
                   pool_scale=_row(pool_scale[j]), conv_w=rnn_conv_w[j], conv_b=_row(rnn_conv_b[j]),
                   wa=_block_diag(rg_wa[j]).astype(BF16), ba=_row(rg_ba[j]),
                   wx=_block_diag(rg_wx[j]).astype(BF16), bx=_row(rg_bx[j]),
                   lam=_row(rg_lambda[j]), w_out=w_out_even[j].astype(BF16),
                   ln_g=_row(ln_mix_g[2 * j]), ln_b=_row(ln_mix_b[2 * j]))
              for j in range(n_even)]
    odd_w = [dict(w_glu=w_glu[j].astype(BF16), dw_w=dw_w[j], dw_b=_row(dw_b[j]),
                  cln_g=_row(conv_ln_g[j]), cln_b=_row(conv_ln_b[j]), w_pw=w_pw[j].astype(BF16),
                  ln_g=_row(ln_mix_g[2 * j + 1]), ln_b=_row(ln_mix_b[2 * j + 1]))
             for j in range(n_odd)]
    peer_w = [dict(wqt=peer_wq[l].T.astype(BF16), k1=peer_k1[l], k2=peer_k2[l],
                   u=_bf16_row_pair_words(peer_u[l]), vt=_bf16_row_pair_words(peer_v[l].T),
                   ln_g=_row(ln_ffn_g[l]), ln_b=_row(ln_ffn_b[l]))
              for l in range(DEPTH)]

    bp = x_prompt.shape[0]
    dt = x_prompt.dtype
    zeros_like_state = lambda st: jnp.zeros((st.shape[0], bp) + st.shape[2:], dt)
    y_p, pool_p, rconv_p, h_p, conv_p = _trunk(
        x_prompt, zeros_like_state(state_pool), zeros_like_state(state_rnn_conv),
        zeros_like_state(state_rnn_h), zeros_like_state(state_conv), 0, even_w, odd_w, peer_w)
    y_s, pool_s, rconv_s, h_s, conv_s = _trunk(
        x_sample, state_pool, state_rnn_conv, state_rnn_h, state_conv, PAST_LEN, even_w, odd_w, peer_w)
    return (y_p, y_s, pool_p, rconv_p, h_p, conv_p, pool_s, rconv_s, h_s, conv_s)
```
